```python
import math
import jax, jax.numpy as jnp
from jax import lax
import numpy as np

D_MODEL = 2048
BATCH = 4
SEQ = 2048
DEPTH = 2
DEC_BATCH = 8
DEC_SEQ = 8
PAST_LEN = 16384
PAGE_SIZE = 128

N_MIXERS = 2
N_ATTN_LAYERS = (DEPTH + 1) // 2
N_POOL_LAYERS = DEPTH // 2
N_HEADS = 8
HEAD_DIM = D_MODEL // N_HEADS // 2
V_DIM = 2 * HEAD_DIM
N_META = 16
POOL_WINDOWS = (2, 4, 8, 16)
N_POOL_GROUPS = len(POOL_WINDOWS)
POOL_GROUP_DIM = D_MODEL // N_POOL_GROUPS
POOL_STATE = max(POOL_WINDOWS) - 1
D_FF = ((8 * D_MODEL + 3 * 256 - 1) // (3 * 256)) * 256
Q_BLOCK = 128
RMS_EPS = 1e-6
SUBLN_EPS = 1e-5
NEG_INF = -1e30

kernel_name = "hybrid_diffattn_pool_decoder_step"


def rmsnorm(x, g, eps=RMS_EPS):
    xf = x.astype(jnp.float32)
    y = xf * lax.rsqrt(jnp.mean(xf * xf, axis=-1, keepdims=True) + eps)
    return (y * g.astype(jnp.float32)).astype(x.dtype)


def alibi_slopes():
    return jnp.asarray(2.0 ** (-8.0 * (np.arange(N_HEADS) + 1) / N_HEADS), dtype=jnp.float32)


def lambda_init(layer_idx):
    return 0.8 - 0.6 * math.exp(-0.3 * layer_idx)


def diff_qkv(u, w_qkv):
    b, t, _ = u.shape
    q, k, v = jnp.split(u @ w_qkv, 3, axis=-1)
    return (q.reshape(b, t, N_HEADS, 2, HEAD_DIM),
            k.reshape(b, t, N_HEADS, 2, HEAD_DIM),
            v.reshape(b, t, N_HEADS, V_DIM))


def diff_attend(q, q_pos, segs, lam, slopes):
    qf = q.astype(jnp.float32) * (HEAD_DIM ** -0.5)
    scores = []
    for k, _, kp in segs:
        s = jnp.einsum('bthjd,bshjd->bhjts', qf, k.astype(jnp.float32))
        dist = (q_pos[:, None] - kp[None, :]).astype(jnp.float32)
        s = jnp.where(dist >= 0, s - slopes[None, :, None, None, None] * dist, NEG_INF)
        scores.append(s)
    p = jax.nn.softmax(jnp.concatenate(scores, axis=-1), axis=-1)
    a = p[:, :, 0] - lam * p[:, :, 1]
    out = 0.0
    off = 0
    for _, v, kp in segs:
        n = kp.shape[0]
        out = out + jnp.einsum('bhts,bshe->bthe', a[..., off:off + n], v.astype(jnp.float32))
        off += n
    return out


def diff_out(o, subln_g, lam_init, w_o, dtype):
    b, t = o.shape[:2]
    o = rmsnorm(o, subln_g, SUBLN_EPS) * (1.0 - lam_init)
    return o.reshape(b, t, D_MODEL).astype(dtype) @ w_o


def pool_mix(u, past, n_prev, w_pool, scale):
    b, t, _ = u.shape
    uf = u.astype(jnp.float32)
    ext = jnp.concatenate([past.astype(jnp.float32), uf], axis=1)
    cs = jnp.concatenate([jnp.zeros((b, 1, D_MODEL), jnp.float32), jnp.cumsum(ext, axis=1)], axis=1)
    pos = jnp.arange(t)
    means = []
    for g, w in enumerate(POOL_WINDOWS):
        sl = slice(g * POOL_GROUP_DIM, (g + 1) * POOL_GROUP_DIM)
        hi = cs[:, POOL_STATE + 1:, sl]
        lo = cs[:, POOL_STATE + 1 - w:POOL_STATE + 1 - w + t, sl]
        cnt = jnp.minimum(w, n_prev + pos + 1).astype(jnp.float32)
        means.append((hi - lo) / cnt[None, :, None])
    p = jnp.concatenate(means, axis=-1) - uf
    y = jnp.einsum('btgc,gcd->btgd', p.reshape(b, t, N_POOL_GROUPS, POOL_GROUP_DIM),
                   w_pool.astype(jnp.float32)).reshape(b, t, D_MODEL) * scale.astype(jnp.float32)
    return y.astype(u.dtype), ext[:, -POOL_STATE:].astype(u.dtype)


def swiglu(u, w_gate_up, w_down):
    g, up = jnp.split(u @ w_gate_up, 2, axis=-1)
    return (jax.nn.silu(g) * up) @ w_down


def setup_inputs(seed: int = 0) -> dict:
    key = jax.random.key(seed)
    ks = jax.random.split(key, 24)
    n_pages = PAST_LEN // PAGE_SIZE
    n_pool_pages = (DEC_BATCH * n_pages * 5 + 3) // 4
    f32 = jnp.float32
    nrm = lambda k, s, sc: jax.random.normal(k, s, f32) * sc
    page_table = jax.random.permutation(ks[5], n_pool_pages)[:DEC_BATCH * n_pages]
    page_table = page_table.reshape(DEC_BATCH, n_pages).astype(jnp.int32)
    return {
        "x_prompt": nrm(ks[0], (BATCH, SEQ, D_MODEL), 1.0),
        "x_sample": nrm(ks[1], (DEC_BATCH, DEC_SEQ, D_MODEL), 1.0),
        "cache_k": nrm(ks[2], (N_ATTN_LAYERS, n_pool_pages, PAGE_SIZE, 2 * N_HEADS, HEAD_DIM), 1.0),
        "cache_v": nrm(ks[3], (N_ATTN_LAYERS, n_pool_pages, PAGE_SIZE, N_HEADS, V_DIM), 1.0),
        "state_pool": nrm(ks[4], (N_POOL_LAYERS, DEC_BATCH, POOL_STATE, D_MODEL), 1.0),
        "page_table": page_table,
        "meta_tokens": nrm(ks[6], (N_META, D_MODEL), 1.0),
        "mix_norm": 1.0 + nrm(ks[7], (DEPTH, D_MODEL), 0.1),
        "ffn_norm": 1.0 + nrm(ks[8], (DEPTH, D_MODEL), 0.1),
        "final_norm": 1.0 + nrm(ks[9], (D_MODEL,), 0.1),
        "w_qkv": nrm(ks[10], (N_ATTN_LAYERS, D_MODEL, 3 * D_MODEL), D_MODEL ** -0.5),
        "lambda_q1": nrm(ks[11], (N_ATTN_LAYERS, HEAD_DIM), 0.1),
        "lambda_k1": nrm(ks[12], (N_ATTN_LAYERS, HEAD_DIM), 0.1),
        "lambda_q2": nrm(ks[13], (N_ATTN_LAYERS, HEAD_DIM), 0.1),
        "lambda_k2": nrm(ks[14], (N_ATTN_LAYERS, HEAD_DIM), 0.1),
        "subln_gain": 1.0 + nrm(ks[15], (N_ATTN_LAYERS, V_DIM), 0.1),
        "w_o": nrm(ks[16], (N_ATTN_LAYERS, D_MODEL, D_MODEL), D_MODEL ** -0.5),
        "w_pool": nrm(ks[17], (N_POOL_LAYERS, N_POOL_GROUPS, POOL_GROUP_DIM, POOL_GROUP_DIM), POOL_GROUP_DIM ** -0.5),
        "pool_scale": 1.0 + nrm(ks[18], (N_POOL_LAYERS, D_MODEL), 0.1),
        "w_gate_up": nrm(ks[19], (DEPTH, D_MODEL, 2 * D_FF), D_MODEL ** -0.5),
        "w_down": nrm(ks[20], (DEPTH, D_FF, D_MODEL), D_FF ** -0.5),
    }


def reference(x_prompt, x_sample, cache_k, cache_v, state_pool, page_table, meta_tokens,
              mix_norm, ffn_norm, final_norm, w_qkv, lambda_q1, lambda_k1, lambda_q2, lambda_k2,
              subln_gain, w_o, w_pool, pool_scale, w_gate_up, w_down):
    b_p = x_prompt.shape[0]
    b_s, t_s = x_sample.shape[:2]
    past_len = page_table.shape[1] * cache_k.shape[2]
    slopes = alibi_slopes()

    meta = jnp.broadcast_to(meta_tokens.astype(x_prompt.dtype)[None], (b_p, N_META, D_MODEL))
    hp = jnp.concatenate([meta, x_prompt], axis=1)
    hs = x_sample
    l_p = hp.shape[1]
    pos_p = jnp.arange(l_p)
    n_blk = (l_p - N_META) // Q_BLOCK
    pos_past = jnp.arange(past_len)
    pos_new = past_len + jnp.arange(t_s)

    kp_rows, vp_rows, ks_rows, vs_rows, poolp_rows, pools_rows = [], [], [], [], [], []
    for i in range(DEPTH):
        up = rmsnorm(hp, mix_norm[i])
        us = rmsnorm(hs, mix_norm[i])
        if i % N_MIXERS == 0:
            a = i // N_MIXERS
            lam_i = lambda_init(i)
            lam = (jnp.exp(jnp.sum(lambda_q1[a].astype(jnp.float32) * lambda_k1[a].astype(jnp.float32)))
                   - jnp.exp(jnp.sum(lambda_q2[a].astype(jnp.float32) * lambda_k2[a].astype(jnp.float32)))
                   + lam_i)
            q, k, v = diff_qkv(up, w_qkv[a])
            o_meta = diff_attend(q[:, :N_META], pos_p[:N_META],
                                 ((k[:, :N_META], v[:, :N_META], pos_p[:N_META]),), lam, slopes)
            q_blocks = jnp.moveaxis(q[:, N_META:].reshape(b_p, n_blk, Q_BLOCK, N_HEADS, 2, HEAD_DIM), 1, 0)
            pos_blocks = pos_p[N_META:].reshape(n_blk, Q_BLOCK)
            o_blocks = lax.map(lambda qb: diff_attend(qb[0], qb[1], ((k, v, pos_p),), lam, slopes),
                               (q_blocks, pos_blocks))
            o_real = jnp.moveaxis(o_blocks, 0, 1).reshape(b_p, l_p - N_META, N_HEADS, V_DIM)
            o = jnp.concatenate([o_meta, o_real], axis=1)
            mp = diff_out(o, subln_gain[a], lam_i, w_o[a], hp.dtype)
            kp_rows.append(k.reshape(b_p, l_p, 2 * N_HEADS, HEAD_DIM))
            vp_rows.append(v)
            qs, kn, vn = diff_qkv(us, w_qkv[a])
            k_past = cache_k[a, page_table].reshape(b_s, past_len, N_HEADS, 2, HEAD_DIM)
            v_past = cache_v[a, page_table].reshape(b_s, past_len, N_HEADS, V_DIM)
            o_s = diff_attend(qs, pos_new, ((k_past, v_past, pos_past), (kn, vn, pos_new)), lam, slopes)
            ms = diff_out(o_s, subln_gain[a], lam_i, w_o[a], hs.dtype)
            ks_rows.append(kn.reshape(b_s, t_s, 2 * N_HEADS, HEAD_DIM))
            vs_rows.append(vn)
        else:
            p = i // N_MIXERS
            mp, st_p = pool_mix(up, jnp.zeros((b_p, POOL_STATE, D_MODEL), up.dtype), 0,
                                w_pool[p], pool_scale[p])
            ms, st_s = pool_mix(us, state_pool[p], past_len, w_pool[p], pool_scale[p])
            poolp_rows.append(st_p)
            pools_rows.append(st_s)
        hp = hp + mp
        hs = hs + ms
        hp = hp + swiglu(rmsnorm(hp, ffn_norm[i]), w_gate_up[i], w_down[i])
        hs = hs + swiglu(rmsnorm(hs, ffn_norm[i]), w_gate_up[i], w_down[i])

    y_prompt = rmsnorm(hp[:, N_META:], final_norm)
    y_sample = rmsnorm(hs, final_norm)
    new_k_prompt = jnp.stack(kp_rows)
    new_v_prompt = jnp.stack(vp_rows)
    new_k_sample = jnp.stack(ks_rows)
    new_v_sample = jnp.stack(vs_rows)
    new_pool_prompt = jnp.stack(poolp_rows)
    new_pool_sample = jnp.stack(pools_rows)
    return (y_prompt, y_sample, new_k_prompt, new_v_prompt, new_k_sample, new_v_sample,
            new_pool_prompt, new_pool_sample)
```

```python
import functools
import math

import jax
import jax.numpy as jnp
import numpy as np
from jax import lax
from jax.experimental import pallas as pl
from jax.experimental.pallas import tpu as pltpu

D_MODEL = 2048
N_HEADS = 8
HEAD_DIM = 128
V_DIM = 2 * HEAD_DIM
N_META = 16
POOL_WINDOWS = (2, 4, 8, 16)
POOL_GROUP_DIM = D_MODEL // len(POOL_WINDOWS)
POOL_HALO = 16
POOL_STATE = max(POOL_WINDOWS) - 1
RMS_EPS = 1e-6
SUBLN_EPS = 1e-5
NEG_INF = -1e30

V7X_VMEM_BYTES = 64 * 1024 * 1024
VMEM_LIMIT = 56 * 1024 * 1024

BF16 = jnp.bfloat16
F32 = jnp.float32


def _lambda_init(layer_idx):
    return 0.8 - 0.6 * math.exp(-0.3 * layer_idx)


def _rms_rows(x, g, eps):
    ms = jnp.mean(x * x, axis=-1, keepdims=True)
    return x * lax.rsqrt(ms + eps) * g


def _lam_value(lam_ref, lam_init):
    a = jnp.sum(lam_ref[0:1, :] * lam_ref[1:2, :], axis=-1, keepdims=True)
    b = jnp.sum(lam_ref[2:3, :] * lam_ref[3:4, :], axis=-1, keepdims=True)
    return jnp.exp(a) - jnp.exp(b) + lam_init


def _dot_nt(a, b):
    return lax.dot_general(a, b, (((1,), (1,)), ((), ())), preferred_element_type=F32)


def _qkv_kernel(x_ref, g_ref, w_ref, q_ref, k_ref, v_ref, kb_ref, vb_ref, u_ref, *, nj, q_scale):
    j = pl.program_id(1)

    @pl.when(j == 0)
    def _():
        u_ref[...] = _rms_rows(x_ref[...], g_ref[...], RMS_EPS).astype(BF16)

    y = jnp.dot(u_ref[...], w_ref[...], preferred_element_type=F32)

    @pl.when(j < nj)
    def _():
        q_ref[...] = (y * q_scale).astype(BF16)

    @pl.when((j >= nj) & (j < 2 * nj))
    def _():
        k_ref[...] = y
        kb_ref[...] = y.astype(BF16)

    @pl.when(j >= 2 * nj)
    def _():
        v_ref[...] = y
        vb_ref[...] = y.astype(BF16)


def _norm_qkv(x, g, w_bf, *, tm, tn=512):
    m = x.shape[0]
    nj = D_MODEL // tn

    def sec(s):
        return lambda i, j: (i, jnp.clip(j - s * nj, 0, nj - 1))

    out_f32 = jax.ShapeDtypeStruct((m, D_MODEL), F32)
    out_bf = jax.ShapeDtypeStruct((m, D_MODEL), BF16)
    return pl.pallas_call(
        functools.partial(_qkv_kernel, nj=nj, q_scale=HEAD_DIM ** -0.5),
        grid=(m // tm, 3 * nj),
        in_specs=[
            pl.BlockSpec((tm, D_MODEL), lambda i, j: (i, 0)),
            pl.BlockSpec((1, D_MODEL), lambda i, j: (0, 0)),
            pl.BlockSpec((D_MODEL, tn), lambda i, j: (0, j)),
        ],
        out_specs=[
            pl.BlockSpec((tm, tn), sec(0)),
            pl.BlockSpec((tm, tn), sec(1)),
            pl.BlockSpec((tm, tn), sec(2)),
            pl.BlockSpec((tm, tn), sec(1)),
            pl.BlockSpec((tm, tn), sec(2)),
        ],
        out_shape=[out_bf, out_f32, out_f32, out_bf, out_bf],
        scratch_shapes=[pltpu.VMEM((tm, D_MODEL), BF16)],
        compiler_params=pltpu.CompilerParams(
            dimension_semantics=("arbitrary", "arbitrary"), vmem_limit_bytes=VMEM_LIMIT),
        name="norm_qkv",
    )(x, g, w_bf)


def _flash_kernel(qi_ref, ki_ref, slopes_ref, q_ref, k_ref, v_ref, km_ref, vm_ref, g_ref, lam_ref,
                  o_ref, m_ref, l_ref, acc_ref, *, tq, tk, lam_init):
    h = pl.program_id(1)
    p = pl.program_id(2)
    qi = qi_ref[p]
    ki = ki_ref[p]
    slope = slopes_ref[h]

    def scores(j, kj, col_off):
        s = _dot_nt(q_ref[0, :, j * HEAD_DIM:(j + 1) * HEAD_DIM], kj)
        col = lax.broadcasted_iota(jnp.int32, (1, kj.shape[0]), 1) + col_off
        return s + slope * col.astype(F32)

    @pl.when(ki == 0)
    def _():
        col = lax.broadcasted_iota(jnp.int32, (1, 128), 1)
        for j in range(2):
            s = scores(j, km_ref[:, j * HEAD_DIM:(j + 1) * HEAD_DIM], -(N_META + qi * tq))
            s = jnp.where(col < N_META, s, NEG_INF)
            m = jnp.max(s, axis=-1, keepdims=True)
            e = jnp.exp(s - m)
            m_ref[j] = m
            l_ref[j] = jnp.sum(e, axis=-1, keepdims=True)
            acc_ref[j] = jnp.dot(e.astype(BF16), vm_ref[...], preferred_element_type=F32)

    def step(masked):
        for j in range(2):
            s = scores(j, k_ref[0, :, j * HEAD_DIM:(j + 1) * HEAD_DIM], ki * tk - qi * tq)
            if masked:
                row = lax.broadcasted_iota(jnp.int32, (tq, tk), 0)
                col = lax.broadcasted_iota(jnp.int32, (tq, tk), 1)
                s = jnp.where(col + (ki * tk - qi * tq) <= row, s, NEG_INF)
            m_old = m_ref[j]
            m_new = jnp.maximum(m_old, jnp.max(s, axis=-1, keepdims=True))
            alpha = jnp.exp(m_old - m_new)
            e = jnp.exp(s - m_new)
            l_ref[j] = alpha * l_ref[j] + jnp.sum(e, axis=-1, keepdims=True)
            acc_ref[j] = alpha * acc_ref[j] + jnp.dot(e.astype(BF16), v_ref[0],
                                                      preferred_element_type=F32)
            m_ref[j] = m_new

    @pl.when(ki < qi)
    def _():
        step(False)

    @pl.when(ki == qi)
    def _():
        step(True)
        lam = _lam_value(lam_ref, lam_init)
        o = acc_ref[0] / l_ref[0] - lam * (acc_ref[1] / l_ref[1])
        o_ref[0] = (_rms_rows(o, g_ref[...], SUBLN_EPS) * (1.0 - lam_init)).astype(BF16)


def _flash_attention(q, k, v, k_meta, v_meta, slopes, subln_g, lam_rows, lam_init, *, tq=512):
    b, t, _ = q.shape
    tk = tq
    nq = t // tq
    pairs = [(qi, ki) for qi in range(nq) for ki in range(qi + 1)]
    qi_tbl = jnp.asarray([p[0] for p in pairs], jnp.int32)
    ki_tbl = jnp.asarray([p[1] for p in pairs], jnp.int32)
    grid_spec = pltpu.PrefetchScalarGridSpec(
        num_scalar_prefetch=3,
        grid=(b, N_HEADS, len(pairs)),
        in_specs=[
            pl.BlockSpec((1, tq, V_DIM), lambda b, h, p, qt, kt, sl: (b, qt[p], h)),
            pl.BlockSpec((1, tk, V_DIM), lambda b, h, p, qt, kt, sl: (b, kt[p], h)),
            pl.BlockSpec((1, tk, V_DIM), lambda b, h, p, qt, kt, sl: (b, kt[p], h)),
            pl.BlockSpec((128, V_DIM), lambda b, h, p, qt, kt, sl: (0, h)),
            pl.BlockSpec((128, V_DIM), lambda b, h, p, qt, kt, sl: (0, h)),
            pl.BlockSpec((1, V_DIM), lambda b, h, p, qt, kt, sl: (0, 0)),
            pl.BlockSpec((4, HEAD_DIM), lambda b, h, p, qt, kt, sl: (0, 0)),
        ],
        out_specs=pl.BlockSpec((1, tq, V_DIM), lambda b, h, p, qt, kt, sl: (b, qt[p], h)),
        scratch_shapes=[
            pltpu.VMEM((2, tq, 1), F32),
            pltpu.VMEM((2, tq, 1), F32),
            pltpu.VMEM((2, tq, V_DIM), F32),
        ],
    )
    return pl.pallas_call(
        functools.partial(_flash_kernel, tq=tq, tk=tk, lam_init=lam_init),
        grid_spec=grid_spec,
        out_shape=jax.ShapeDtypeStruct((b, t, D_MODEL), BF16),
        compiler_params=pltpu.CompilerParams(
            dimension_semantics=("arbitrary", "arbitrary", "arbitrary")),
        name="flash_diff_attn",
    )(qi_tbl, ki_tbl, slopes, q, k, v, k_meta, v_meta, subln_g, lam_rows)


def _meta_attn_kernel(slopes_ref, q_ref, k_ref, v_ref, g_ref, lam_ref, o_ref, *, lam_init):
    h = pl.program_id(0)
    slope = slopes_ref[h]
    row = lax.broadcasted_iota(jnp.int32, (N_META, 128), 0)
    col = lax.broadcasted_iota(jnp.int32, (N_META, 128), 1)
    probs = []
    for j in range(2):
        sl = slice(j * HEAD_DIM, (j + 1) * HEAD_DIM)
        s = _dot_nt(q_ref[:, sl], k_ref[:, sl]) - slope * (row - col).astype(F32)
        s = jnp.where(col <= row, s, NEG_INF)
        e = jnp.exp(s - jnp.max(s, axis=-1, keepdims=True))
        probs.append(e / jnp.sum(e, axis=-1, keepdims=True))
    lam = _lam_value(lam_ref, lam_init)
    a = probs[0] - lam * probs[1]
    o = jnp.dot(a.astype(BF16), v_ref[...], preferred_element_type=F32)
    o_ref[...] = _rms_rows(o, g_ref[...], SUBLN_EPS) * (1.0 - lam_init)


def _meta_attention(q, k, v, slopes, subln_g, lam_rows, lam_init):
    blk = pl.BlockSpec((N_META, V_DIM), lambda h, sl: (0, h))
    kv_blk = pl.BlockSpec((128, V_DIM), lambda h, sl: (0, h))
    grid_spec = pltpu.PrefetchScalarGridSpec(
        num_scalar_prefetch=1,
        grid=(N_HEADS,),
        in_specs=[blk, kv_blk, kv_blk,
                  pl.BlockSpec((1, V_DIM), lambda h, sl: (0, 0)),
                  pl.BlockSpec((4, HEAD_DIM), lambda h, sl: (0, 0))],
        out_specs=blk,
    )
    return pl.pallas_call(
        functools.partial(_meta_attn_kernel, lam_init=lam_init),
        grid_spec=grid_spec,
        out_shape=jax.ShapeDtypeStruct((N_META, D_MODEL), F32),
        compiler_params=pltpu.CompilerParams(dimension_semantics=("arbitrary",)),
        name="meta_attn",
    )(slopes, q, k, v, subln_g, lam_rows)


def _paged_kernel(pt_ref, qbd_ref, *refs, pp, page, past_len, n_new, lam_init):
    k_refs = refs[:pp]
    v_refs = refs[pp:2 * pp]
    kn_ref, vn_ref, slope_ref, g_ref, lam_ref, o_ref, m_ref, l_ref, acc_ref = refs[2 * pp:]
    s_idx = pl.program_id(1)
    n_rows = 2 * N_HEADS * n_new
    rows_per_head = 2 * n_new

    @pl.when(s_idx == 0)
    def _():
        m_ref[...] = jnp.full_like(m_ref, NEG_INF)
        l_ref[...] = jnp.zeros_like(l_ref)
        acc_ref[...] = jnp.zeros_like(acc_ref)

    qbd = qbd_ref[0]
    slope = slope_ref[...]

    def online_update(s, v_blocks):
        m_old = m_ref[...]
        m_new = jnp.maximum(m_old, jnp.max(s, axis=-1, keepdims=True))
        alpha = jnp.exp(m_old - m_new)
        e = jnp.exp(s - m_new)
        l_ref[...] = alpha * l_ref[...] + jnp.sum(e, axis=-1, keepdims=True)
        m_ref[...] = m_new
        eb = e.astype(BF16)
        for h in range(N_HEADS):
            rs = slice(h * rows_per_head, (h + 1) * rows_per_head)
            acc = alpha[rs] * acc_ref[rs, :]
            for off, width, load in v_blocks:
                acc = acc + jnp.dot(eb[rs, off:off + width], load(h), preferred_element_type=F32)
            acc_ref[rs, :] = acc

    s_parts = []
    for i in range(pp):
        kb = k_refs[i][0].astype(BF16)
        col = lax.broadcasted_iota(jnp.int32, (1, page), 1) + ((s_idx * pp + i) * page - past_len)
        s_parts.append(_dot_nt(qbd, kb) + slope * col.astype(F32))
    s_all = jnp.concatenate(s_parts, axis=1) if pp > 1 else s_parts[0]

    def page_loader(i):
        return lambda h: v_refs[i][0, :, h * V_DIM:(h + 1) * V_DIM].astype(BF16)

    online_update(s_all, [(i * page, page, page_loader(i)) for i in range(pp)])

    @pl.when(s_idx == pl.num_programs(1) - 1)
    def _():
        n_pad = kn_ref.shape[1]
        col = lax.broadcasted_iota(jnp.int32, (n_rows, n_pad), 1)
        qrow = lax.broadcasted_iota(jnp.int32, (n_rows, n_pad), 0) & (n_new - 1)
        s = _dot_nt(qbd, kn_ref[0]) + slope * col.astype(F32)
        s = jnp.where(col <= qrow, s, NEG_INF)
        online_update(s, [(0, n_pad, lambda h: vn_ref[0, :, h * V_DIM:(h + 1) * V_DIM])])

        lam = _lam_value(lam_ref, lam_init)
        inv_l = 1.0 / l_ref[...]
        for h in range(N_HEADS):
            r0 = h * rows_per_head
            a1 = acc_ref[r0:r0 + n_new, :] * inv_l[r0:r0 + n_new]
            a2 = acc_ref[r0 + n_new:r0 + 2 * n_new, :] * inv_l[r0 + n_new:r0 + 2 * n_new]
            o = a1 - lam * a2
            o_ref[0, :, h * V_DIM:(h + 1) * V_DIM] = _rms_rows(o, g_ref[...], SUBLN_EPS) * (1.0 - lam_init)


def _paged_attention(qbd, cache_k, cache_v, page_table, kn, vn, slope_rows, subln_g, lam_rows,
                     lam_init, *, pp=4):
    bs = qbd.shape[0]
    n_pages = page_table.shape[1]
    page = cache_k.shape[1]
    n_new = 8
    n_pad = kn.shape[1]

    def page_spec(i):
        return pl.BlockSpec((1, page, D_MODEL), lambda b, s, pt: (pt[b, s * pp + i], 0, 0))

    const2 = lambda b, s, pt: (0, 0)
    per_b = lambda b, s, pt: (b, 0, 0)
    grid_spec = pltpu.PrefetchScalarGridSpec(
        num_scalar_prefetch=1,
        grid=(bs, n_pages // pp),
        in_specs=([pl.BlockSpec((1, 2 * N_HEADS * n_new, D_MODEL), per_b)]
                  + [page_spec(i) for i in range(pp)]
                  + [page_spec(i) for i in range(pp)]
                  + [pl.BlockSpec((1, n_pad, D_MODEL), per_b),
                     pl.BlockSpec((1, n_pad, D_MODEL), per_b),
                     pl.BlockSpec((2 * N_HEADS * n_new, 1), const2),
                     pl.BlockSpec((1, V_DIM), const2),
                     pl.BlockSpec((4, HEAD_DIM), const2)]),
        out_specs=pl.BlockSpec((1, n_new, D_MODEL), per_b),
        scratch_shapes=[
            pltpu.VMEM((2 * N_HEADS * n_new, 1), F32),
            pltpu.VMEM((2 * N_HEADS * n_new, 1), F32),
            pltpu.VMEM((2 * N_HEADS * n_new, V_DIM), F32),
        ],
    )
    return pl.pallas_call(
        functools.partial(_paged_kernel, pp=pp, page=page, past_len=n_pages * page, n_new=n_new,
                          lam_init=lam_init),
        grid_spec=grid_spec,
        out_shape=jax.ShapeDtypeStruct((bs, n_new, D_MODEL), F32),
        compiler_params=pltpu.CompilerParams(
            dimension_semantics=("arbitrary", "arbitrary"), vmem_limit_bytes=VMEM_LIMIT),
        name="paged_diff_attn",
    )(page_table, qbd, *([cache_k] * pp), *([cache_v] * pp), kn, vn, slope_rows, subln_g, lam_rows)


def _mm_res_kernel(x_ref, w_ref, r_ref, o_ref):
    o_ref[...] = r_ref[...] + jnp.dot(x_ref[...], w_ref[...], preferred_element_type=F32)


def _matmul_residual(x_bf, w_bf, res, *, tm, tn=512):
    m, kdim = x_bf.shape
    n = w_bf.shape[1]
    return pl.pallas_call(
        _mm_res_kernel,
        grid=(m // tm, n // tn),
        in_specs=[
            pl.BlockSpec((tm, kdim), lambda i, j: (i, 0)),
            pl.BlockSpec((kdim, tn), lambda i, j: (0, j)),
            pl.BlockSpec((tm, tn), lambda i, j: (i, j)),
        ],
        out_specs=pl.BlockSpec((tm, tn), lambda i, j: (i, j)),
        out_shape=jax.ShapeDtypeStruct((m, n), F32),
        compiler_params=pltpu.CompilerParams(
            dimension_semantics=("arbitrary", "arbitrary"), vmem_limit_bytes=VMEM_LIMIT),
        name="matmul_residual",
    )(x_bf, w_bf, res)


def _ffn_kernel(x_ref, g_ref, wg_ref, wu_ref, wd_ref, fg_ref, o_ref, u_ref, acc_ref, *, final_norm):
    j = pl.program_id(1)

    @pl.when(j == 0)
    def _():
        x = x_ref[...]
        u_ref[...] = _rms_rows(x, g_ref[...], RMS_EPS).astype(BF16)
        acc_ref[...] = x

    u = u_ref[...]
    gate = jnp.dot(u, wg_ref[...], preferred_element_type=F32)
    up = jnp.dot(u, wu_ref[...], preferred_element_type=F32)
    act = (gate / (1.0 + jnp.exp(-gate)) * up).astype(BF16)
    acc_ref[...] += jnp.dot(act, wd_ref[...], preferred_element_type=F32)

    @pl.when(j == pl.num_programs(1) - 1)
    def _():
        y = acc_ref[...]
        if final_norm:
            y = _rms_rows(y, fg_ref[...], RMS_EPS)
        o_ref[...] = y


def _ffn(x, g, w_gu_bf, w_d_bf, fg, *, tm, tf=512, final_norm):
    m = x.shape[0]
    d_ff = w_d_bf.shape[0]
    nf = d_ff // tf
    return pl.pallas_call(
        functools.partial(_ffn_kernel, final_norm=final_norm),
        grid=(m // tm, nf),
        in_specs=[
            pl.BlockSpec((tm, D_MODEL), lambda i, j: (i, 0)),
            pl.BlockSpec((1, D_MODEL), lambda i, j: (0, 0)),
            pl.BlockSpec((D_MODEL, tf), lambda i, j: (0, j)),
            pl.BlockSpec((D_MODEL, tf), lambda i, j: (0, j + nf)),
            pl.BlockSpec((tf, D_MODEL), lambda i, j: (j, 0)),
            pl.BlockSpec((1, D_MODEL), lambda i, j: (0, 0)),
        ],
        out_specs=pl.BlockSpec((tm, D_MODEL), lambda i, j: (i, 0)),
        out_shape=jax.ShapeDtypeStruct((m, D_MODEL), F32),
        scratch_shapes=[pltpu.VMEM((tm, D_MODEL), BF16), pltpu.VMEM((tm, D_MODEL), F32)],
        compiler_params=pltpu.CompilerParams(
            dimension_semantics=("arbitrary", "arbitrary"), vmem_limit_bytes=VMEM_LIMIT),
        name="ffn_swiglu",
    )(x, g, w_gu_bf, w_gu_bf, w_d_bf, fg)


def _pool_kernel(*refs, tm, n_tiles, n_prev):
    if n_tiles > 1:
        h_ref, halo_ref, past_ref, g_ref, w_ref, sc_ref, o_ref, st_ref, ext_ref = refs
    else:
        h_ref, past_ref, g_ref, w_ref, sc_ref, o_ref, st_ref, ext_ref = refs
        halo_ref = None
    i = pl.program_id(1)
    g = g_ref[...]
    x = h_ref[0]
    ext_ref[POOL_HALO:POOL_HALO + tm, :] = _rms_rows(x, g, RMS_EPS)
    if n_tiles > 1:
        @pl.when(i == 0)
        def _():
            ext_ref[0:POOL_HALO, :] = past_ref[0]

        @pl.when(i > 0)
        def _():
            ext_ref[0:POOL_HALO, :] = _rms_rows(halo_ref[0], g, RMS_EPS)
    else:
        ext_ref[0:POOL_HALO, :] = past_ref[0]

    pos = i * tm + lax.broadcasted_iota(jnp.int32, (tm, 1), 0)
    for gi, w in enumerate(POOL_WINDOWS):
        cols = slice(gi * POOL_GROUP_DIM, (gi + 1) * POOL_GROUP_DIM)
        u = ext_ref[POOL_HALO:POOL_HALO + tm, cols]
        win = u
        for k in range(1, w):
            win = win + ext_ref[POOL_HALO - k:POOL_HALO - k + tm, cols]
        if n_prev >= w - 1:
            mean = win * (1.0 / w)
        else:
            mean = win / jnp.minimum(w, n_prev + pos + 1).astype(F32)
        y = jnp.dot((mean - u).astype(BF16), w_ref[gi], preferred_element_type=F32)
        o_ref[0, :, cols] = x[:, cols] + y * sc_ref[:, cols]

    @pl.when(i == n_tiles - 1)
    def _():
        st_ref[0] = ext_ref[tm + POOL_HALO - POOL_STATE:tm + POOL_HALO, :]


def _pool_mix(h, past, g, w_bf, scale, *, tm, n_prev):
    b, t, _ = h.shape
    n_tiles = t // tm
    hb = tm // POOL_HALO
    in_specs = [pl.BlockSpec((1, tm, D_MODEL), lambda b, i: (b, i, 0))]
    args = [h]
    if n_tiles > 1:
        in_specs.append(pl.BlockSpec((1, POOL_HALO, D_MODEL),
                                     lambda b, i: (b, jnp.maximum(i * hb - 1, 0), 0)))
        args.append(h)
    in_specs += [
        pl.BlockSpec((1, POOL_HALO, D_MODEL), lambda b, i: (b, 0, 0)),
        pl.BlockSpec((1, D_MODEL), lambda b, i: (0, 0)),
        pl.BlockSpec((len(POOL_WINDOWS), POOL_GROUP_DIM, POOL_GROUP_DIM), lambda b, i: (0, 0, 0)),
        pl.BlockSpec((1, D_MODEL), lambda b, i: (0, 0)),
    ]
    args += [past, g, w_bf, scale]
    return pl.pallas_call(
        functools.partial(_pool_kernel, tm=tm, n_tiles=n_tiles, n_prev=n_prev),
        grid=(b, n_tiles),
        in_specs=in_specs,
        out_specs=[
            pl.BlockSpec((1, tm, D_MODEL), lambda b, i: (b, i, 0)),
            pl.BlockSpec((1, POOL_STATE, D_MODEL), lambda b, i: (b, 0, 0)),
        ],
        out_shape=[jax.ShapeDtypeStruct((b, t, D_MODEL), F32),
                   jax.ShapeDtypeStruct((b, POOL_STATE, D_MODEL), F32)],
        scratch_shapes=[pltpu.VMEM((tm + POOL_HALO, D_MODEL), F32)],
        compiler_params=pltpu.CompilerParams(
            dimension_semantics=("arbitrary", "arbitrary"), vmem_limit_bytes=VMEM_LIMIT),
        name="pool_mix",
    )(*args)


def kernel(x_prompt, x_sample, cache_k, cache_v, state_pool, page_table, meta_tokens, mix_norm, ffn_norm, final_norm, w_qkv, lambda_q1, lambda_k1, lambda_q2, lambda_k2, subln_gain, w_o, w_pool, pool_scale, w_gate_up, w_down):
    b_p, seq, _ = x_prompt.shape
    b_s, t_s, _ = x_sample.shape
    n_pages = page_table.shape[1]
    page = cache_k.shape[2]
    past_len = n_pages * page
    m_main = b_p * seq
    n_small = N_META + b_s * t_s
    tm_main = 1024

    slopes = jnp.asarray(2.0 ** (-8.0 * (np.arange(N_HEADS) + 1) / N_HEADS), F32)
    slope_rows = jnp.repeat(slopes, 2 * t_s).reshape(2 * N_HEADS * t_s, 1)
    row = lambda a: a.reshape(1, -1).astype(F32)

    hm = x_prompt.reshape(m_main, D_MODEL)
    hs = jnp.concatenate([meta_tokens.astype(x_prompt.dtype), x_sample.reshape(b_s * t_s, D_MODEL)], axis=0)

    a = 0
    lam_init = _lambda_init(0)
    lam_rows = jnp.stack([lambda_q1[a], lambda_k1[a], lambda_q2[a], lambda_k2[a]]).astype(F32)
    subln_g = row(subln_gain[a])
    w_qkv_bf = w_qkv[a].astype(BF16)
    w_o_bf = w_o[a].astype(BF16)
    g_mix0 = row(mix_norm[0])

    q_m, k_m, v_m, kb_m, vb_m = _norm_qkv(hm, g_mix0, w_qkv_bf, tm=tm_main)
    q_s, k_s, v_s, kb_s, vb_s = _norm_qkv(hs, g_mix0, w_qkv_bf, tm=n_small)

    pad_meta = lambda x: jnp.pad(x[:N_META], ((0, 128 - N_META), (0, 0)))
    k_meta_pad, v_meta_pad = pad_meta(kb_s), pad_meta(vb_s)
    o_m = _flash_attention(q_m.reshape(b_p, seq, D_MODEL), kb_m.reshape(b_p, seq, D_MODEL),
                           vb_m.reshape(b_p, seq, D_MODEL), k_meta_pad, v_meta_pad,
                           slopes, subln_g, lam_rows, lam_init)
    o_meta = _meta_attention(q_s[:N_META], k_meta_pad, v_meta_pad, slopes, subln_g, lam_rows, lam_init)

    qs = q_s[N_META:].reshape(b_s, t_s, 2 * N_HEADS, HEAD_DIM)
    eye = jnp.eye(2 * N_HEADS, dtype=BF16)
    qbd = jnp.einsum('bqsd,st->bsqtd', qs, eye).reshape(b_s, 2 * N_HEADS * t_s, D_MODEL)
    pad_new = lambda x: jnp.pad(x[N_META:].reshape(b_s, t_s, D_MODEL), ((0, 0), (0, 128 - t_s), (0, 0)))
    o_s = _paged_attention(qbd, cache_k[a].reshape(-1, page, D_MODEL), cache_v[a].reshape(-1, page, D_MODEL),
                           page_table, pad_new(kb_s), pad_new(vb_s), slope_rows, subln_g, lam_rows, lam_init)
    o_small = jnp.concatenate([o_meta, o_s.reshape(b_s * t_s, D_MODEL)], axis=0).astype(BF16)

    hm = _matmul_residual(o_m.reshape(m_main, D_MODEL), w_o_bf, hm, tm=tm_main)
    hs = _matmul_residual(o_small, w_o_bf, hs, tm=n_small)

    w_gu0 = w_gate_up[0].astype(BF16)
    w_d0 = w_down[0].astype(BF16)
    g_ffn0 = row(ffn_norm[0])
    fg = row(final_norm)
    hm = _ffn(hm, g_ffn0, w_gu0, w_d0, fg, tm=512, final_norm=False)
    hs = _ffn(hs, g_ffn0, w_gu0, w_d0, fg, tm=n_small, final_norm=False)

    p = 0
    g_mix1 = row(mix_norm[1])
    w_pool_bf = w_pool[p].astype(BF16)
    sc = row(pool_scale[p])
    zero_row = lambda n: jnp.zeros((n, 1, D_MODEL), F32)
    h_meta, st_meta = _pool_mix(hs[:N_META].reshape(1, N_META, D_MODEL), jnp.zeros((1, POOL_HALO, D_MODEL), F32),
                                g_mix1, w_pool_bf, sc, tm=N_META, n_prev=0)
    h_smp, st_smp = _pool_mix(hs[N_META:].reshape(b_s, t_s, D_MODEL),
                              jnp.concatenate([zero_row(b_s), state_pool[p].astype(F32)], axis=1),
                              g_mix1, w_pool_bf, sc, tm=t_s, n_prev=past_len)
    past_main = jnp.broadcast_to(jnp.concatenate([zero_row(1), st_meta], axis=1), (b_p, POOL_HALO, D_MODEL))
    hm3, st_main = _pool_mix(hm.reshape(b_p, seq, D_MODEL), past_main, g_mix1, w_pool_bf, sc,
                             tm=512, n_prev=N_META)
    hm = hm3.reshape(m_main, D_MODEL)
    hs = jnp.concatenate([h_meta.reshape(N_META, D_MODEL), h_smp.reshape(b_s * t_s, D_MODEL)], axis=0)

    w_gu1 = w_gate_up[1].astype(BF16)
    w_d1 = w_down[1].astype(BF16)
    g_ffn1 = row(ffn_norm[1])
    ym = _ffn(hm, g_ffn1, w_gu1, w_d1, fg, tm=512, final_norm=True)
    ys = _ffn(hs, g_ffn1, w_gu1, w_d1, fg, tm=n_small, final_norm=True)

    def with_meta(real, meta):
        meta_b = jnp.broadcast_to(meta[None], (b_p, N_META, D_MODEL))
        return jnp.concatenate([meta_b, real.reshape(b_p, seq, D_MODEL)], axis=1)

    new_k_prompt = with_meta(k_m, k_s[:N_META]).reshape(1, b_p, N_META + seq, 2 * N_HEADS, HEAD_DIM)
    new_v_prompt = with_meta(v_m, v_s[:N_META]).reshape(1, b_p, N_META + seq, N_HEADS, V_DIM)
    new_k_sample = k_s[N_META:].reshape(1, b_s, t_s, 2 * N_HEADS, HEAD_DIM)
    new_v_sample = v_s[N_META:].reshape(1, b_s, t_s, N_HEADS, V_DIM)
    return (ym.reshape(b_p, seq, D_MODEL), ys[N_META:].reshape(b_s, t_s, D_MODEL),
            new_k_prompt, new_v_prompt, new_k_sample, new_v_sample,
            st_main[None], st_smp[None])
```

```python
import functools
import math

import jax
import jax.numpy as jnp
import numpy as np
from jax import lax
from jax.experimental import pallas as pl
from jax.experimental.pallas import tpu as pltpu

D_MODEL = 2048
N_HEADS = 8
N_SUB = 2 * N_HEADS
HEAD_DIM = 128
V_DIM = 2 * HEAD_DIM
N_META = 16
LANES = 128
POOL_WINDOWS = (2, 4, 8, 16)
POOL_GROUP_DIM = D_MODEL // len(POOL_WINDOWS)
POOL_HALO = 16
POOL_STATE = max(POOL_WINDOWS) - 1
RMS_EPS = 1e-6
SUBLN_EPS = 1e-5
NEG_INF = -1e30

VMEM_LIMIT = 56 * 1024 * 1024

BF16 = jnp.bfloat16
F32 = jnp.float32


def _lambda_init(layer_idx):
    return 0.8 - 0.6 * math.exp(-0.3 * layer_idx)


def _rms_rows(x, g, eps):
    ms = jnp.mean(x * x, axis=-1, keepdims=True)
    return x * lax.rsqrt(ms + eps) * g


def _lam_value(lam_ref, lam_init):
    a = jnp.sum(lam_ref[0:1, :] * lam_ref[1:2, :], axis=-1, keepdims=True)
    b = jnp.sum(lam_ref[2:3, :] * lam_ref[3:4, :], axis=-1, keepdims=True)
    return jnp.exp(a) - jnp.exp(b) + lam_init


def _dot_nt(a, b):
    return lax.dot_general(a, b, (((1,), (1,)), ((), ())), preferred_element_type=F32)


def _qkv_kernel(x_ref, g_ref, w_ref, q_ref, k_ref, v_ref, kb_ref, vb_ref, u_ref, *, nj, q_scale):
    j = pl.program_id(1)

    @pl.when(j == 0)
    def _():
        u_ref[...] = _rms_rows(x_ref[...], g_ref[...], RMS_EPS).astype(BF16)

    y = jnp.dot(u_ref[...], w_ref[...], preferred_element_type=F32)

    @pl.when(j < nj)
    def _():
        q_ref[...] = (y * q_scale).astype(BF16)

    @pl.when((j >= nj) & (j < 2 * nj))
    def _():
        k_ref[...] = y
        kb_ref[...] = y.astype(BF16)

    @pl.when(j >= 2 * nj)
    def _():
        v_ref[...] = y
        vb_ref[...] = y.astype(BF16)


def _norm_qkv(x, g, w_bf, *, tm, tn=512):
    m = x.shape[0]
    nj = D_MODEL // tn

    def sec(s):
        return lambda i, j: (i, jnp.clip(j - s * nj, 0, nj - 1))

    out_f32 = jax.ShapeDtypeStruct((m, D_MODEL), F32)
    out_bf = jax.ShapeDtypeStruct((m, D_MODEL), BF16)
    return pl.pallas_call(
        functools.partial(_qkv_kernel, nj=nj, q_scale=HEAD_DIM ** -0.5),
        grid=(m // tm, 3 * nj),
        in_specs=[
            pl.BlockSpec((tm, D_MODEL), lambda i, j: (i, 0)),
            pl.BlockSpec((1, D_MODEL), lambda i, j: (0, 0)),
            pl.BlockSpec((D_MODEL, tn), lambda i, j: (0, j)),
        ],
        out_specs=[
            pl.BlockSpec((tm, tn), sec(0)),
            pl.BlockSpec((tm, tn), sec(1)),
            pl.BlockSpec((tm, tn), sec(2)),
            pl.BlockSpec((tm, tn), sec(1)),
            pl.BlockSpec((tm, tn), sec(2)),
        ],
        out_shape=[out_bf, out_f32, out_f32, out_bf, out_bf],
        scratch_shapes=[pltpu.VMEM((tm, D_MODEL), BF16)],
        compiler_params=pltpu.CompilerParams(
            dimension_semantics=("arbitrary", "arbitrary"), vmem_limit_bytes=VMEM_LIMIT),
        name="norm_qkv",
    )(x, g, w_bf)


def _flash_kernel(qi_ref, ki_ref, slopes_ref, q_ref, k_ref, vt_ref, km_ref, vmt_ref, g_ref, lam_ref,
                  o_ref, m_ref, l_ref, acc_ref, *, tq, tk, lam_init):
    h = pl.program_id(1)
    p = pl.program_id(2)
    qi = qi_ref[p]
    ki = ki_ref[p]
    slope = slopes_ref[h]

    def key_bias(n, off, n_valid=None):
        r = lax.broadcasted_iota(jnp.int32, (n, LANES), 0)
        b = slope * (r + off).astype(F32)
        if n_valid is not None:
            b = jnp.where(r < n_valid, b, NEG_INF)
        return jnp.concatenate([b] * (tq // LANES), axis=1)

    def scores_t(j, kj):
        return _dot_nt(kj, q_ref[0, :, j * HEAD_DIM:(j + 1) * HEAD_DIM])

    @pl.when(ki == 0)
    def _():
        bias = key_bias(LANES, -(N_META + qi * tq), N_META)
        for j in range(2):
            s = scores_t(j, km_ref[:, j * HEAD_DIM:(j + 1) * HEAD_DIM]) + bias
            m = jnp.max(s, axis=0, keepdims=True)
            e = jnp.exp(s - m)
            m_ref[j] = m
            l_ref[j] = jnp.sum(e, axis=0, keepdims=True)
            acc_ref[j] = jnp.dot(vmt_ref[...], e.astype(BF16), preferred_element_type=F32)

    def step(masked):
        bias = key_bias(tk, ki * tk - qi * tq)
        if masked:
            row = lax.broadcasted_iota(jnp.int32, (tk, tq), 0)
            col = lax.broadcasted_iota(jnp.int32, (tk, tq), 1)
            visible = row + (ki * tk - qi * tq) <= col
        for j in range(2):
            s = scores_t(j, k_ref[0, :, j * HEAD_DIM:(j + 1) * HEAD_DIM]) + bias
            if masked:
                s = jnp.where(visible, s, NEG_INF)
            m_old = m_ref[j]
            m_new = jnp.maximum(m_old, jnp.max(s, axis=0, keepdims=True))
            alpha = jnp.exp(m_old - m_new)
            e = jnp.exp(s - m_new)
            l_ref[j] = alpha * l_ref[j] + jnp.sum(e, axis=0, keepdims=True)
            acc_ref[j] = alpha * acc_ref[j] + jnp.dot(vt_ref[0], e.astype(BF16),
                                                      preferred_element_type=F32)
            m_ref[j] = m_new

    @pl.when(ki < qi)
    def _():
        step(False)

    @pl.when(ki == qi)
    def _():
        step(True)
        lam = _lam_value(lam_ref, lam_init)
        o_t = acc_ref[0] / l_ref[0] - lam * (acc_ref[1] / l_ref[1])
        o_ref[0] = (_rms_rows(o_t.T, g_ref[...], SUBLN_EPS) * (1.0 - lam_init)).astype(BF16)


def _flash_attention(q, k, v_t, k_meta, v_meta_t, slopes, subln_g, lam_rows, lam_init, *, tq=512):
    b, t, _ = q.shape
    tk = tq
    nq = t // tq
    pairs = [(qi, ki) for qi in range(nq) for ki in range(qi + 1)]
    qi_tbl = jnp.asarray([p[0] for p in pairs], jnp.int32)
    ki_tbl = jnp.asarray([p[1] for p in pairs], jnp.int32)
    grid_spec = pltpu.PrefetchScalarGridSpec(
        num_scalar_prefetch=3,
        grid=(b, N_HEADS, len(pairs)),
        in_specs=[
            pl.BlockSpec((1, tq, V_DIM), lambda b, h, p, qt, kt, sl: (b, qt[p], h)),
            pl.BlockSpec((1, tk, V_DIM), lambda b, h, p, qt, kt, sl: (b, kt[p], h)),
            pl.BlockSpec((1, V_DIM, tk), lambda b, h, p, qt, kt, sl: (b, h, kt[p])),
            pl.BlockSpec((LANES, V_DIM), lambda b, h, p, qt, kt, sl: (0, h)),
            pl.BlockSpec((V_DIM, LANES), lambda b, h, p, qt, kt, sl: (h, 0)),
            pl.BlockSpec((1, V_DIM), lambda b, h, p, qt, kt, sl: (0, 0)),
            pl.BlockSpec((4, HEAD_DIM), lambda b, h, p, qt, kt, sl: (0, 0)),
        ],
        out_specs=pl.BlockSpec((1, tq, V_DIM), lambda b, h, p, qt, kt, sl: (b, qt[p], h)),
        scratch_shapes=[
            pltpu.VMEM((2, 1, tq), F32),
            pltpu.VMEM((2, 1, tq), F32),
            pltpu.VMEM((2, V_DIM, tq), F32),
        ],
    )
    return pl.pallas_call(
        functools.partial(_flash_kernel, tq=tq, tk=tk, lam_init=lam_init),
        grid_spec=grid_spec,
        out_shape=jax.ShapeDtypeStruct((b, t, D_MODEL), BF16),
        compiler_params=pltpu.CompilerParams(
            dimension_semantics=("arbitrary", "arbitrary", "arbitrary")),
        name="flash_diff_attn",
    )(qi_tbl, ki_tbl, slopes, q, k, v_t, k_meta, v_meta_t, subln_g, lam_rows)


def _meta_attn_kernel(slopes_ref, q_ref, k_ref, v_ref, g_ref, lam_ref, o_ref, *, lam_init):
    h = pl.program_id(0)
    slope = slopes_ref[h]
    row = lax.broadcasted_iota(jnp.int32, (N_META, LANES), 0)
    col = lax.broadcasted_iota(jnp.int32, (N_META, LANES), 1)
    probs = []
    for j in range(2):
        sl = slice(j * HEAD_DIM, (j + 1) * HEAD_DIM)
        s = _dot_nt(q_ref[:, sl], k_ref[:, sl]) - slope * (row - col).astype(F32)
        s = jnp.where(col <= row, s, NEG_INF)
        e = jnp.exp(s - jnp.max(s, axis=-1, keepdims=True))
        probs.append(e / jnp.sum(e, axis=-1, keepdims=True))
    lam = _lam_value(lam_ref, lam_init)
    a = probs[0] - lam * probs[1]
    o = jnp.dot(a.astype(BF16), v_ref[...], preferred_element_type=F32)
    o_ref[...] = _rms_rows(o, g_ref[...], SUBLN_EPS) * (1.0 - lam_init)


def _meta_attention(q, k, v, slopes, subln_g, lam_rows, lam_init):
    blk = pl.BlockSpec((N_META, V_DIM), lambda h, sl: (0, h))
    kv_blk = pl.BlockSpec((LANES, V_DIM), lambda h, sl: (0, h))
    grid_spec = pltpu.PrefetchScalarGridSpec(
        num_scalar_prefetch=1,
        grid=(N_HEADS,),
        in_specs=[blk, kv_blk, kv_blk,
                  pl.BlockSpec((1, V_DIM), lambda h, sl: (0, 0)),
                  pl.BlockSpec((4, HEAD_DIM), lambda h, sl: (0, 0))],
        out_specs=blk,
    )
    return pl.pallas_call(
        functools.partial(_meta_attn_kernel, lam_init=lam_init),
        grid_spec=grid_spec,
        out_shape=jax.ShapeDtypeStruct((N_META, D_MODEL), F32),
        compiler_params=pltpu.CompilerParams(dimension_semantics=("arbitrary",)),
        name="meta_attn",
    )(slopes, q, k, v, subln_g, lam_rows)


def _paged_kernel(pt_ref, q_ref, *refs, pp, page, past_len, n_new, lam_init):
    k_refs = refs[:pp]
    v_refs = refs[pp:3 * pp]
    kn_ref, vn_ref, slope_ref, g_ref, lam_ref, o_ref, m_ref, l_ref, acc_ref = refs[3 * pp:]
    s_idx = pl.program_id(1)
    rows_per_head = 2 * n_new
    n_rows = N_HEADS * rows_per_head

    @pl.when(s_idx == 0)
    def _():
        m_ref[...] = jnp.full_like(m_ref, NEG_INF)
        l_ref[...] = jnp.zeros_like(l_ref)
        acc_ref[...] = jnp.zeros_like(acc_ref)

    slope = slope_ref[...]

    def scores(key_loader):
        parts = []
        for sub in range(N_SUB):
            r = _dot_nt(q_ref[0, sub * 16:(sub + 1) * 16, :], key_loader(sub))
            parts.append(r[:n_new])
        return jnp.concatenate(parts, axis=0)

    def online_update(s, v_blocks):
        m_old = m_ref[...]
        m_new = jnp.maximum(m_old, jnp.max(s, axis=-1, keepdims=True))
        alpha = jnp.exp(m_old - m_new)
        e = jnp.exp(s - m_new)
        l_ref[...] = alpha * l_ref[...] + jnp.sum(e, axis=-1, keepdims=True)
        m_ref[...] = m_new
        eb = e.astype(BF16)
        for h in range(N_HEADS):
            rs = slice(h * rows_per_head, (h + 1) * rows_per_head)
            acc = alpha[rs] * acc_ref[rs, :]
            for off, width, load in v_blocks:
                acc = acc + jnp.dot(eb[rs, off:off + width], load(h), preferred_element_type=F32)
            acc_ref[rs, :] = acc

    s_parts = []
    for i in range(pp):
        k_page = k_refs[i]
        col = lax.broadcasted_iota(jnp.int32, (1, page), 1) + ((s_idx * pp + i) * page - past_len)
        raw = scores(lambda sub: k_page[pl.ds(sub, page, stride=N_SUB), :].astype(BF16))
        s_parts.append(raw + slope * col.astype(F32))
    s_all = jnp.concatenate(s_parts, axis=1) if pp > 1 else s_parts[0]

    def page_loader(i):
        def load(h):
            halves = [v_refs[2 * i + half][pl.ds(h, page, stride=N_HEADS), :].astype(BF16) for half in range(2)]
            return jnp.concatenate(halves, axis=1)
        return load

    online_update(s_all, [(i * page, page, page_loader(i)) for i in range(pp)])

    @pl.when(s_idx == pl.num_programs(1) - 1)
    def _():
        n_pad = kn_ref.shape[1]
        col = lax.broadcasted_iota(jnp.int32, (n_rows, n_pad), 1)
        qrow = lax.broadcasted_iota(jnp.int32, (n_rows, n_pad), 0) & (n_new - 1)
        s = scores(lambda sub: kn_ref[0, :, sub * HEAD_DIM:(sub + 1) * HEAD_DIM]) + slope * col.astype(F32)
        s = jnp.where(col <= qrow, s, NEG_INF)
        online_update(s, [(0, n_pad, lambda h: vn_ref[0, :, h * V_DIM:(h + 1) * V_DIM])])

        lam = _lam_value(lam_ref, lam_init)
        inv_l = 1.0 / l_ref[...]
        for h in range(N_HEADS):
            r0 = h * rows_per_head
            a1 = acc_ref[r0:r0 + n_new, :] * inv_l[r0:r0 + n_new]
            a2 = acc_ref[r0 + n_new:r0 + 2 * n_new, :] * inv_l[r0 + n_new:r0 + 2 * n_new]
            o = a1 - lam * a2
            o_ref[0, :, h * V_DIM:(h + 1) * V_DIM] = _rms_rows(o, g_ref[...], SUBLN_EPS) * (1.0 - lam_init)


def _paged_attention(q_pad, cache_k, cache_v, page_base, page_table, kn, vn, slope_rows, subln_g, lam_rows,
                     lam_init, *, pp=4):
    bs = q_pad.shape[0]
    n_pages = page_table.shape[1]
    page = cache_k.shape[1] // N_SUB
    n_new = 8
    n_pad = kn.shape[1]
    n_rows = N_SUB * n_new

    def k_spec(i):
        return pl.BlockSpec((None, page * N_SUB, HEAD_DIM),
                            lambda b, s, pt: (page_base + pt[b, s * pp + i], 0, 0))

    def v_spec(i, half):
        return pl.BlockSpec((None, page * N_HEADS, LANES),
                            lambda b, s, pt: (page_base + pt[b, s * pp + i], 0, half))

    const2 = lambda b, s, pt: (0, 0)
    per_b = lambda b, s, pt: (b, 0, 0)
    grid_spec = pltpu.PrefetchScalarGridSpec(
        num_scalar_prefetch=1,
        grid=(bs, n_pages // pp),
        in_specs=([pl.BlockSpec((1, N_SUB * 16, HEAD_DIM), per_b)]
                  + [k_spec(i) for i in range(pp)]
                  + [v_spec(i, half) for i in range(pp) for half in range(2)]
                  + [pl.BlockSpec((1, n_pad, D_MODEL), per_b),
                     pl.BlockSpec((1, n_pad, D_MODEL), per_b),
                     pl.BlockSpec((n_rows, 1), const2),
                     pl.BlockSpec((1, V_DIM), const2),
                     pl.BlockSpec((4, HEAD_DIM), const2)]),
        out_specs=pl.BlockSpec((1, n_new, D_MODEL), per_b),
        scratch_shapes=[
            pltpu.VMEM((n_rows, 1), F32),
            pltpu.VMEM((n_rows, 1), F32),
            pltpu.VMEM((n_rows, V_DIM), F32),
        ],
    )
    return pl.pallas_call(
        functools.partial(_paged_kernel, pp=pp, page=page, past_len=n_pages * page, n_new=n_new,
                          lam_init=lam_init),
        grid_spec=grid_spec,
        out_shape=jax.ShapeDtypeStruct((bs, n_new, D_MODEL), F32),
        compiler_params=pltpu.CompilerParams(
            dimension_semantics=("arbitrary", "arbitrary"), vmem_limit_bytes=VMEM_LIMIT),
        name="paged_diff_attn",
    )(page_table, q_pad, *([cache_k] * pp), *([cache_v] * (2 * pp)), kn, vn, slope_rows, subln_g, lam_rows)


def _mm_res_kernel(x_ref, w_ref, r_ref, o_ref):
    o_ref[...] = r_ref[...] + jnp.dot(x_ref[...], w_ref[...], preferred_element_type=F32)


def _matmul_residual(x_bf, w_bf, res, *, tm, tn=512):
    m, kdim = x_bf.shape
    n = w_bf.shape[1]
    return pl.pallas_call(
        _mm_res_kernel,
        grid=(m // tm, n // tn),
        in_specs=[
            pl.BlockSpec((tm, kdim), lambda i, j: (i, 0)),
            pl.BlockSpec((kdim, tn), lambda i, j: (0, j)),
            pl.BlockSpec((tm, tn), lambda i, j: (i, j)),
        ],
        out_specs=pl.BlockSpec((tm, tn), lambda i, j: (i, j)),
        out_shape=jax.ShapeDtypeStruct((m, n), F32),
        compiler_params=pltpu.CompilerParams(
            dimension_semantics=("arbitrary", "arbitrary"), vmem_limit_bytes=VMEM_LIMIT),
        name="matmul_residual",
    )(x_bf, w_bf, res)


def _ffn_kernel(x_ref, g_ref, wg_ref, wu_ref, wd_ref, fg_ref, o_ref, u_ref, acc_ref, *, final_norm):
    j = pl.program_id(1)

    @pl.when(j == 0)
    def _():
        x = x_ref[...]
        u_ref[...] = _rms_rows(x, g_ref[...], RMS_EPS).astype(BF16)
        acc_ref[...] = x

    u = u_ref[...]
    gate = jnp.dot(u, wg_ref[...], preferred_element_type=F32)
    up = jnp.dot(u, wu_ref[...], preferred_element_type=F32)
    act = (gate / (1.0 + jnp.exp(-gate)) * up).astype(BF16)
    acc_ref[...] += jnp.dot(act, wd_ref[...], preferred_element_type=F32)

    @pl.when(j == pl.num_programs(1) - 1)
    def _():
        y = acc_ref[...]
        if final_norm:
            y = _rms_rows(y, fg_ref[...], RMS_EPS)
        o_ref[...] = y


def _ffn(x, g, w_gu_bf, w_d_bf, fg, *, tm, tf=512, final_norm):
    m = x.shape[0]
    d_ff = w_d_bf.shape[0]
    nf = d_ff // tf
    return pl.pallas_call(
        functools.partial(_ffn_kernel, final_norm=final_norm),
        grid=(m // tm, nf),
        in_specs=[
            pl.BlockSpec((tm, D_MODEL), lambda i, j: (i, 0)),
            pl.BlockSpec((1, D_MODEL), lambda i, j: (0, 0)),
            pl.BlockSpec((D_MODEL, tf), lambda i, j: (0, j)),
            pl.BlockSpec((D_MODEL, tf), lambda i, j: (0, j + nf)),
            pl.BlockSpec((tf, D_MODEL), lambda i, j: (j, 0)),
            pl.BlockSpec((1, D_MODEL), lambda i, j: (0, 0)),
        ],
        out_specs=pl.BlockSpec((tm, D_MODEL), lambda i, j: (i, 0)),
        out_shape=jax.ShapeDtypeStruct((m, D_MODEL), F32),
        scratch_shapes=[pltpu.VMEM((tm, D_MODEL), BF16), pltpu.VMEM((tm, D_MODEL), F32)],
        compiler_params=pltpu.CompilerParams(
            dimension_semantics=("arbitrary", "arbitrary"), vmem_limit_bytes=VMEM_LIMIT),
        name="ffn_swiglu",
    )(x, g, w_gu_bf, w_gu_bf, w_d_bf, fg)


def _pool_kernel(*refs, tm, n_tiles, n_prev):
    if n_tiles > 1:
        h_ref, halo_ref, past_ref, g_ref, w_ref, sc_ref, o_ref, st_ref, ext_ref = refs
    else:
        h_ref, past_ref, g_ref, w_ref, sc_ref, o_ref, st_ref, ext_ref = refs
        halo_ref = None
    i = pl.program_id(1)
    g = g_ref[...]
    x = h_ref[0]
    ext_ref[POOL_HALO:POOL_HALO + tm, :] = _rms_rows(x, g, RMS_EPS)
    if n_tiles > 1:
        @pl.when(i == 0)
        def _():
            ext_ref[0:POOL_HALO, :] = past_ref[0]

        @pl.when(i > 0)
        def _():
            ext_ref[0:POOL_HALO, :] = _rms_rows(halo_ref[0], g, RMS_EPS)
    else:
        ext_ref[0:POOL_HALO, :] = past_ref[0]

    pos = i * tm + lax.broadcasted_iota(jnp.int32, (tm, 1), 0)
    for gi, w in enumerate(POOL_WINDOWS):
        cols = slice(gi * POOL_GROUP_DIM, (gi + 1) * POOL_GROUP_DIM)
        u = ext_ref[POOL_HALO:POOL_HALO + tm, cols]
        win = u
        for k in range(1, w):
            win = win + ext_ref[POOL_HALO - k:POOL_HALO - k + tm, cols]
        if n_prev >= w - 1:
            mean = win * (1.0 / w)
        else:
            mean = win / jnp.minimum(w, n_prev + pos + 1).astype(F32)
        y = jnp.dot((mean - u).astype(BF16), w_ref[gi], preferred_element_type=F32)
        o_ref[0, :, cols] = x[:, cols] + y * sc_ref[:, cols]

    @pl.when(i == n_tiles - 1)
    def _():
        st_ref[0] = ext_ref[tm + POOL_HALO - POOL_STATE:tm + POOL_HALO, :]


def _pool_mix(h, past, g, w_bf, scale, *, tm, n_prev):
    b, t, _ = h.shape
    n_tiles = t // tm
    hb = tm // POOL_HALO
    in_specs = [pl.BlockSpec((1, tm, D_MODEL), lambda b, i: (b, i, 0))]
    args = [h]
    if n_tiles > 1:
        in_specs.append(pl.BlockSpec((1, POOL_HALO, D_MODEL),
                                     lambda b, i: (b, jnp.maximum(i * hb - 1, 0), 0)))
        args.append(h)
    in_specs += [
        pl.BlockSpec((1, POOL_HALO, D_MODEL), lambda b, i: (b, 0, 0)),
        pl.BlockSpec((1, D_MODEL), lambda b, i: (0, 0)),
        pl.BlockSpec((len(POOL_WINDOWS), POOL_GROUP_DIM, POOL_GROUP_DIM), lambda b, i: (0, 0, 0)),
        pl.BlockSpec((1, D_MODEL), lambda b, i: (0, 0)),
    ]
    args += [past, g, w_bf, scale]
    return pl.pallas_call(
        functools.partial(_pool_kernel, tm=tm, n_tiles=n_tiles, n_prev=n_prev),
        grid=(b, n_tiles),
        in_specs=in_specs,
        out_specs=[
            pl.BlockSpec((1, tm, D_MODEL), lambda b, i: (b, i, 0)),
            pl.BlockSpec((1, POOL_STATE, D_MODEL), lambda b, i: (b, 0, 0)),
        ],
        out_shape=[jax.ShapeDtypeStruct((b, t, D_MODEL), F32),
                   jax.ShapeDtypeStruct((b, POOL_STATE, D_MODEL), F32)],
        scratch_shapes=[pltpu.VMEM((tm + POOL_HALO, D_MODEL), F32)],
        compiler_params=pltpu.CompilerParams(
            dimension_semantics=("arbitrary", "arbitrary"), vmem_limit_bytes=VMEM_LIMIT),
        name="pool_mix",
    )(*args)


def kernel(x_prompt, x_sample, cache_k, cache_v, state_pool, page_table, meta_tokens, mix_norm, ffn_norm, final_norm, w_qkv, lambda_q1, lambda_k1, lambda_q2, lambda_k2, subln_gain, w_o, w_pool, pool_scale, w_gate_up, w_down):
    b_p, seq, _ = x_prompt.shape
    b_s, t_s, _ = x_sample.shape
    n_pages = page_table.shape[1]
    n_pool_pages, page = cache_k.shape[1], cache_k.shape[2]
    past_len = n_pages * page
    m_main = b_p * seq
    n_small = N_META + b_s * t_s
    tm_main = 1024

    slopes = jnp.asarray(2.0 ** (-8.0 * (np.arange(N_HEADS) + 1) / N_HEADS), F32)
    slope_rows = jnp.repeat(slopes, 2 * t_s).reshape(2 * N_HEADS * t_s, 1)
    row = lambda a: a.reshape(1, -1).astype(F32)

    hm = x_prompt.reshape(m_main, D_MODEL)
    hs = jnp.concatenate([meta_tokens.astype(x_prompt.dtype), x_sample.reshape(b_s * t_s, D_MODEL)], axis=0)

    a = 0
    lam_init = _lambda_init(0)
    lam_rows = jnp.stack([lambda_q1[a], lambda_k1[a], lambda_q2[a], lambda_k2[a]]).astype(F32)
    subln_g = row(subln_gain[a])
    w_qkv_bf = w_qkv[a].astype(BF16)
    w_o_bf = w_o[a].astype(BF16)
    g_mix0 = row(mix_norm[0])

    q_m, k_m, v_m, kb_m, vb_m = _norm_qkv(hm, g_mix0, w_qkv_bf, tm=tm_main)
    q_s, k_s, v_s, kb_s, vb_s = _norm_qkv(hs, g_mix0, w_qkv_bf, tm=n_small)

    pad_meta = lambda x: jnp.pad(x[:N_META], ((0, LANES - N_META), (0, 0)))
    k_meta_pad, v_meta_pad = pad_meta(kb_s), pad_meta(vb_s)
    v_t = jnp.swapaxes(vb_m.reshape(b_p, seq, D_MODEL), 1, 2)
    o_m = _flash_attention(q_m.reshape(b_p, seq, D_MODEL), kb_m.reshape(b_p, seq, D_MODEL), v_t,
                           k_meta_pad, v_meta_pad.T, slopes, subln_g, lam_rows, lam_init)
    o_meta = _meta_attention(q_s[:N_META], k_meta_pad, v_meta_pad, slopes, subln_g, lam_rows, lam_init)

    qs = q_s[N_META:].reshape(b_s, t_s, N_SUB, HEAD_DIM).transpose(0, 2, 1, 3)
    q_pad = jnp.pad(qs, ((0, 0), (0, 0), (0, 16 - t_s), (0, 0))).reshape(b_s, N_SUB * 16, HEAD_DIM)
    pad_new = lambda x: jnp.pad(x[N_META:].reshape(b_s, t_s, D_MODEL), ((0, 0), (0, LANES - t_s), (0, 0)))
    o_s = _paged_attention(q_pad, cache_k.reshape(-1, page * N_SUB, HEAD_DIM),
                           cache_v.reshape(-1, page * N_HEADS, V_DIM), a * n_pool_pages,
                           page_table, pad_new(kb_s), pad_new(vb_s), slope_rows, subln_g, lam_rows, lam_init)
    o_small = jnp.concatenate([o_meta, o_s.reshape(b_s * t_s, D_MODEL)], axis=0).astype(BF16)

    hm = _matmul_residual(o_m.reshape(m_main, D_MODEL), w_o_bf, hm, tm=tm_main)
    hs = _matmul_residual(o_small, w_o_bf, hs, tm=n_small)

    w_gu0 = w_gate_up[0].astype(BF16)
    w_d0 = w_down[0].astype(BF16)
    g_ffn0 = row(ffn_norm[0])
    fg = row(final_norm)
    hm = _ffn(hm, g_ffn0, w_gu0, w_d0, fg, tm=512, final_norm=False)
    hs = _ffn(hs, g_ffn0, w_gu0, w_d0, fg, tm=n_small, final_norm=False)

    p = 0
    g_mix1 = row(mix_norm[1])
    w_pool_bf = w_pool[p].astype(BF16)
    sc = row(pool_scale[p])
    zero_row = lambda n: jnp.zeros((n, 1, D_MODEL), F32)
    h_meta, st_meta = _pool_mix(hs[:N_META].reshape(1, N_META, D_MODEL), jnp.zeros((1, POOL_HALO, D_MODEL), F32),
                                g_mix1, w_pool_bf, sc, tm=N_META, n_prev=0)
    h_smp, st_smp = _pool_mix(hs[N_META:].reshape(b_s, t_s, D_MODEL),
                              jnp.concatenate([zero_row(b_s), state_pool[p].astype(F32)], axis=1),
                              g_mix1, w_pool_bf, sc, tm=t_s, n_prev=past_len)
    past_main = jnp.broadcast_to(jnp.concatenate([zero_row(1), st_meta], axis=1), (b_p, POOL_HALO, D_MODEL))
    hm3, st_main = _pool_mix(hm.reshape(b_p, seq, D_MODEL), past_main, g_mix1, w_pool_bf, sc,
                             tm=512, n_prev=N_META)
    hm = hm3.reshape(m_main, D_MODEL)
    hs = jnp.concatenate([h_meta.reshape(N_META, D_MODEL), h_smp.reshape(b_s * t_s, D_MODEL)], axis=0)

    w_gu1 = w_gate_up[1].astype(BF16)
    w_d1 = w_down[1].astype(BF16)
    g_ffn1 = row(ffn_norm[1])
    ym = _ffn(hm, g_ffn1, w_gu1, w_d1, fg, tm=512, final_norm=True)
    ys = _ffn(hs, g_ffn1, w_gu1, w_d1, fg, tm=n_small, final_norm=True)

    def with_meta(real, meta):
        meta_b = jnp.broadcast_to(meta[None], (b_p, N_META, D_MODEL))
        return jnp.concatenate([meta_b, real.reshape(b_p, seq, D_MODEL)], axis=1)

    new_k_prompt = with_meta(k_m, k_s[:N_META]).reshape(1, b_p, N_META + seq, 2 * N_HEADS, HEAD_DIM)
    new_v_prompt = with_meta(v_m, v_s[:N_META]).reshape(1, b_p, N_META + seq, N_HEADS, V_DIM)
    new_k_sample = k_s[N_META:].reshape(1, b_s, t_s, 2 * N_HEADS, HEAD_DIM)
    new_v_sample = v_s[N_META:].reshape(1, b_s, t_s, N_HEADS, V_DIM)
    return (ym.reshape(b_p, seq, D_MODEL), ys[N_META:].reshape(b_s, t_s, D_MODEL),
            new_k_prompt, new_v_prompt, new_k_sample, new_v_sample,
            st_main[None], st_smp[None])
```
